```python
import math, functools
import jax, jax.numpy as jnp
from jax import lax
import numpy as np

D_MODEL = 2048
BATCH = 8
SEQ = 2048
DEPTH = 1
DEC_BATCH = 8
DEC_SEQ = 16
PAST_LEN = 1024

CHUNK = 64
Q_BLOCK = 128
N_DIFF_HEADS = D_MODEL // 256
DIFF_QK_DIM = 64
DIFF_V_DIM = 2 * DIFF_QK_DIM
N_SB_HEADS = D_MODEL // 256
SB_HEAD_DIM = 128
DIFF_QK_COLS = N_DIFF_HEADS * 2 * DIFF_QK_DIM
DIFF_WIDTH = N_DIFF_HEADS * DIFF_V_DIM
SB_WIDTH = N_SB_HEADS * SB_HEAD_DIM
MIX_WIDTH = DIFF_WIDTH + SB_WIDTH
IN_SPLITS = (DIFF_QK_COLS, 2 * DIFF_QK_COLS, 2 * DIFF_QK_COLS + DIFF_WIDTH,
             2 * DIFF_QK_COLS + DIFF_WIDTH + SB_WIDTH,
             2 * DIFF_QK_COLS + DIFF_WIDTH + 2 * SB_WIDTH)
IN_COLS = 2 * DIFF_QK_COLS + DIFF_WIDTH + 3 * SB_WIDTH
PEER_HEADS = 8
PEER_N_KEYS = 128
PEER_EXPERTS = PEER_N_KEYS * PEER_N_KEYS
PEER_QUERY_DIM = 256
PEER_SUBKEY_DIM = PEER_QUERY_DIM // 2
PEER_TOPK = 16
PEER_TOKEN_BLOCK = 128
DEEPNORM_ALPHA = (2.0 * DEPTH) ** 0.25
DEEPNORM_BETA = (8.0 * DEPTH) ** -0.25
LN_EPS = 1e-5
SUBLN_EPS = 1e-5

kernel_name = "diff_stickbreak_peer_streaming_step"


def layer_norm(x, g, b):
    xf = x.astype(jnp.float32)
    mu = jnp.mean(xf, axis=-1, keepdims=True)
    var = jnp.mean(jnp.square(xf - mu), axis=-1, keepdims=True)
    return ((xf - mu) * lax.rsqrt(var + LN_EPS) * g + b).astype(x.dtype)


def alibi_slopes(n_heads):
    return jnp.exp2(-8.0 * (jnp.arange(n_heads, dtype=jnp.float32) + 1.0) / n_heads)


def diff_attention(q, q_pos, k, v, k_pos, lam, lam_init, subln_g):
    s = jnp.einsum('bqhcd,bkhcd->bhcqk', q.astype(jnp.float32), k.astype(jnp.float32)) * (DIFF_QK_DIM ** -0.5)
    dist = jnp.abs(q_pos[:, None] - k_pos[None, :]).astype(jnp.float32)
    bias = -alibi_slopes(N_DIFF_HEADS)[:, None, None, None] * dist
    mask = (k_pos[None, :] // CHUNK) <= (q_pos[:, None] // CHUNK)
    s = jnp.where(mask, s + bias, -jnp.inf)
    p = jax.nn.softmax(s, axis=-1)
    w = p[:, :, 0] - lam * p[:, :, 1]
    o = jnp.einsum('bhqk,bkhd->bqhd', w, v.astype(jnp.float32))
    o = o * lax.rsqrt(jnp.mean(o * o, axis=-1, keepdims=True) + SUBLN_EPS) * subln_g * (1.0 - lam_init)
    return o.astype(v.dtype)


def stick_breaking_attention(q, q_pos, k, v, k_pos):
    z = jnp.einsum('bqhd,bkhd->bhqk', q.astype(jnp.float32), k.astype(jnp.float32)) * (SB_HEAD_DIM ** -0.5)
    mask = k_pos[None, :] < q_pos[:, None]
    log_keep = jnp.where(mask, jax.nn.log_sigmoid(-z), 0.0)
    rev = lax.cumsum(log_keep, axis=3, reverse=True)
    later = jnp.concatenate([rev[..., 1:], jnp.zeros_like(rev[..., :1])], axis=-1)
    a = jnp.where(mask, jnp.exp(jax.nn.log_sigmoid(z) + later), 0.0)
    o = jnp.einsum('bhqk,bkhd->bqhd', a, v.astype(jnp.float32))
    return o.astype(v.dtype)


def sweep_query_blocks(fn, q, q_pos, k, v, k_pos):
    b, tq = q.shape[0], q.shape[1]
    if tq <= Q_BLOCK:
        return fn(q, q_pos, k, v, k_pos)
    nb = tq // Q_BLOCK
    qb = jnp.moveaxis(q.reshape((b, nb, Q_BLOCK) + q.shape[2:]), 1, 0)
    pb = q_pos.reshape(nb, Q_BLOCK)
    ob = lax.map(lambda a: fn(a[0], a[1], k, v, k_pos), (qb, pb))
    return jnp.moveaxis(ob, 0, 1).reshape((b, tq) + ob.shape[3:])


def peer_block(xb, w_q, subkeys, u_tab, v_tab):
    n = xb.shape[0]
    q = jnp.einsum('nd,de->ne', xb, w_q).reshape(n, PEER_HEADS, 2, PEER_SUBKEY_DIM)
    s = jnp.einsum('nhcd,hckd->nhck', q.astype(jnp.float32), subkeys.astype(jnp.float32))
    s1, i1 = lax.top_k(s[:, :, 0], PEER_TOPK)
    s2, i2 = lax.top_k(s[:, :, 1], PEER_TOPK)
    cand = (s1[..., :, None] + s2[..., None, :]).reshape(n, PEER_HEADS, PEER_TOPK * PEER_TOPK)
    cidx = (i1[..., :, None] * PEER_N_KEYS + i2[..., None, :]).reshape(n, PEER_HEADS, PEER_TOPK * PEER_TOPK)
    top, pos = lax.top_k(cand, PEER_TOPK)
    eidx = jnp.take_along_axis(cidx, pos, axis=-1)
    g = jax.nn.softmax(top, axis=-1)
    u = u_tab[eidx]
    act = jax.nn.gelu(jnp.einsum('nd,nhkd->nhk', xb, u).astype(jnp.float32), approximate=False) * g
    return jnp.einsum('nhk,nhkd->nd', act.astype(xb.dtype), v_tab[eidx])


def peer_ffn(x, w_q, subkeys, u_tab, v_tab):
    b, t, d = x.shape
    xf = x.reshape(b * t, d)
    n = xf.shape[0]
    pad = (-n) % PEER_TOKEN_BLOCK
    xp = jnp.pad(xf, ((0, pad), (0, 0))).reshape(-1, PEER_TOKEN_BLOCK, d)
    out = lax.map(lambda xb: peer_block(xb, w_q, subkeys, u_tab, v_tab), xp)
    return out.reshape(-1, d)[:n].reshape(b, t, d)


def trunk_layer(x, past, lam_init, w_in, w_out, lam_q1, lam_k1, lam_q2, lam_k2, subln_g,
                ln1_g, ln1_b, ln2_g, ln2_b, peer_w_q, peer_subkeys, peer_u, peer_v):
    b, t, _ = x.shape
    h = jnp.einsum('btd,de->bte', x, w_in)
    qd, kd, vd, qs, ks, vs = jnp.split(h, IN_SPLITS, axis=-1)
    qd = qd.reshape(b, t, N_DIFF_HEADS, 2, DIFF_QK_DIM)
    kd = kd.reshape(b, t, N_DIFF_HEADS, 2, DIFF_QK_DIM)
    vd = vd.reshape(b, t, N_DIFF_HEADS, DIFF_V_DIM)
    qs = qs.reshape(b, t, N_SB_HEADS, SB_HEAD_DIM)
    ks = ks.reshape(b, t, N_SB_HEADS, SB_HEAD_DIM)
    vs = vs.reshape(b, t, N_SB_HEADS, SB_HEAD_DIM)
    new_rows = (kd, vd, ks, vs)
    if past is None:
        past_len = 0
        kd_all, vd_all, ks_all, vs_all = new_rows
    else:
        past_len = past[0].shape[1]
        kd_all, vd_all, ks_all, vs_all = [jnp.concatenate([p, r], axis=1) for p, r in zip(past, new_rows)]
    q_pos = past_len + jnp.arange(t, dtype=jnp.int32)
    k_pos = jnp.arange(past_len + t, dtype=jnp.int32)
    lam = (jnp.exp(jnp.sum(lam_q1.astype(jnp.float32) * lam_k1.astype(jnp.float32)))
           - jnp.exp(jnp.sum(lam_q2.astype(jnp.float32) * lam_k2.astype(jnp.float32))) + lam_init)
    diff_fn = functools.partial(diff_attention, lam=lam, lam_init=lam_init, subln_g=subln_g)
    o_diff = sweep_query_blocks(diff_fn, qd, q_pos, kd_all, vd_all, k_pos)
    o_sb = sweep_query_blocks(stick_breaking_attention, qs, q_pos, ks_all, vs_all, k_pos)
    mixed = jnp.concatenate([o_diff.reshape(b, t, DIFF_WIDTH), o_sb.reshape(b, t, SB_WIDTH)], axis=-1)
    x = layer_norm(DEEPNORM_ALPHA * x + jnp.einsum('bte,ed->btd', mixed, w_out), ln1_g, ln1_b)
    x = layer_norm(DEEPNORM_ALPHA * x + peer_ffn(x, peer_w_q, peer_subkeys, peer_u, peer_v), ln2_g, ln2_b)
    return x, new_rows


def setup_inputs(seed: int = 0) -> dict:
    key = jax.random.key(seed)
    ks = jax.random.split(key, 32)
    f32 = jnp.float32

    def nrm(k, shape, scale):
        return jax.random.normal(k, shape, f32) * scale

    s_in = D_MODEL ** -0.5
    beta = DEEPNORM_BETA
    w_in = jnp.concatenate([
        nrm(ks[6], (DEPTH, D_MODEL, DIFF_QK_COLS), s_in),
        nrm(ks[7], (DEPTH, D_MODEL, DIFF_QK_COLS), s_in),
        nrm(ks[8], (DEPTH, D_MODEL, DIFF_WIDTH), s_in * beta),
        nrm(ks[9], (DEPTH, D_MODEL, SB_WIDTH), s_in),
        nrm(ks[10], (DEPTH, D_MODEL, SB_WIDTH), s_in),
        nrm(ks[11], (DEPTH, D_MODEL, SB_WIDTH), s_in * beta),
    ], axis=-1)
    return {
        "x_prompt": nrm(ks[0], (BATCH, SEQ, D_MODEL), 1.0),
        "x_sample": nrm(ks[1], (DEC_BATCH, DEC_SEQ, D_MODEL), 1.0),
        "cache_diff_k": nrm(ks[2], (DEPTH, DEC_BATCH, PAST_LEN, N_DIFF_HEADS, 2, DIFF_QK_DIM), 1.0),
        "cache_diff_v": nrm(ks[3], (DEPTH, DEC_BATCH, PAST_LEN, N_DIFF_HEADS, DIFF_V_DIM), beta),
        "cache_sb_k": nrm(ks[4], (DEPTH, DEC_BATCH, PAST_LEN, N_SB_HEADS, SB_HEAD_DIM), 1.0),
        "cache_sb_v": nrm(ks[5], (DEPTH, DEC_BATCH, PAST_LEN, N_SB_HEADS, SB_HEAD_DIM), beta),
        "w_in": w_in,
        "w_out": nrm(ks[12], (DEPTH, MIX_WIDTH, D_MODEL), MIX_WIDTH ** -0.5 * beta),
        "lam_q1": nrm(ks[13], (DEPTH, DIFF_QK_DIM), 0.1),
        "lam_k1": nrm(ks[14], (DEPTH, DIFF_QK_DIM), 0.1),
        "lam_q2": nrm(ks[15], (DEPTH, DIFF_QK_DIM), 0.1),
        "lam_k2": nrm(ks[16], (DEPTH, DIFF_QK_DIM), 0.1),
        "subln_g": 1.0 + nrm(ks[17], (DEPTH, DIFF_V_DIM), 0.02),
        "ln1_g": 1.0 + nrm(ks[18], (DEPTH, D_MODEL), 0.02),
        "ln1_b": nrm(ks[19], (DEPTH, D_MODEL), 0.02),
        "ln2_g": 1.0 + nrm(ks[20], (DEPTH, D_MODEL), 0.02),
        "ln2_b": nrm(ks[21], (DEPTH, D_MODEL), 0.02),
        "peer_w_q": nrm(ks[22], (DEPTH, D_MODEL, PEER_HEADS * PEER_QUERY_DIM), s_in),
        "peer_subkeys": nrm(ks[23], (DEPTH, PEER_HEADS, 2, PEER_N_KEYS, PEER_SUBKEY_DIM), PEER_SUBKEY_DIM ** -0.5),
        "peer_u": nrm(ks[24], (DEPTH, PEER_EXPERTS, D_MODEL), s_in * beta),
        "peer_v": nrm(ks[25], (DEPTH, PEER_EXPERTS, D_MODEL), beta),
    }


def reference(x_prompt, x_sample, cache_diff_k, cache_diff_v, cache_sb_k, cache_sb_v,
              w_in, w_out, lam_q1, lam_k1, lam_q2, lam_k2, subln_g,
              ln1_g, ln1_b, ln2_g, ln2_b, peer_w_q, peer_subkeys, peer_u, peer_v):
    y_prompt, y_sample = x_prompt, x_sample
    rows_p, rows_s = [], []
    for l in range(DEPTH):
        lam_init = 0.8 - 0.6 * math.exp(-0.3 * l)
        params = (w_in[l], w_out[l], lam_q1[l], lam_k1[l], lam_q2[l], lam_k2[l], subln_g[l],
                  ln1_g[l], ln1_b[l], ln2_g[l], ln2_b[l], peer_w_q[l], peer_subkeys[l], peer_u[l], peer_v[l])
        y_prompt, rp = trunk_layer(y_prompt, None, lam_init, *params)
        past = (cache_diff_k[l], cache_diff_v[l], cache_sb_k[l], cache_sb_v[l])
        y_sample, rs = trunk_layer(y_sample, past, lam_init, *params)
        rows_p.append(rp)
        rows_s.append(rs)
    new_diff_k_prompt = jnp.stack([r[0] for r in rows_p])
    new_diff_v_prompt = jnp.stack([r[1] for r in rows_p])
    new_sb_k_prompt = jnp.stack([r[2] for r in rows_p])
    new_sb_v_prompt = jnp.stack([r[3] for r in rows_p])
    new_diff_k_sample = jnp.stack([r[0] for r in rows_s])
    new_diff_v_sample = jnp.stack([r[1] for r in rows_s])
    new_sb_k_sample = jnp.stack([r[2] for r in rows_s])
    new_sb_v_sample = jnp.stack([r[3] for r in rows_s])
    return (y_prompt, y_sample, new_diff_k_prompt, new_diff_v_prompt, new_sb_k_prompt, new_sb_v_prompt,
            new_diff_k_sample, new_diff_v_sample, new_sb_k_sample, new_sb_v_sample)
```

```python
import functools
import math

import jax
import jax.numpy as jnp
from jax import lax
from jax.experimental import pallas as pl
from jax.experimental.pallas import tpu as pltpu

F32 = jnp.float32
BF16 = jnp.bfloat16
I32 = jnp.int32

D_MODEL = 2048
CHUNK = 64
CHUNK_SHIFT = 6
N_HEADS = 8
HEAD_W = 128
DIFF_QK_DIM = 64
SB_HEAD_DIM = 128
GROUP_W = N_HEADS * HEAD_W
PEER_HEADS = 8
PEER_N_KEYS = 128
PEER_TOPK = 16
PEER_PICKS = PEER_HEADS * PEER_TOPK
DEPTH = 1
DEEPNORM_ALPHA = (2.0 * DEPTH) ** 0.25
LN_EPS = 1e-5
SUBLN_EPS = 1e-5

LANES = 128
SUBLANES = 8
D_TILES = D_MODEL // LANES
ROW_TILES = 2 * D_TILES
ROW_PITCH = ROW_TILES + 4
VMEM_LIMIT = 56 * 1024 * 1024


def _cparams(n_axes):
    return pltpu.CompilerParams(dimension_semantics=("arbitrary",) * n_axes,
                                vmem_limit_bytes=VMEM_LIMIT)


def _mm_kernel(x_ref, w_ref, o_ref):
    o_ref[...] = jnp.dot(x_ref[...].astype(BF16), w_ref[...],
                         preferred_element_type=F32).astype(o_ref.dtype)


def _matmul(x, w, col_block, width, out_dtype, tm):
    n, k = x.shape
    tm = min(tm, n)
    return pl.pallas_call(
        _mm_kernel,
        grid=(n // tm,),
        in_specs=[pl.BlockSpec((tm, k), lambda i: (i, 0)),
                  pl.BlockSpec((k, width), lambda i: (0, col_block))],
        out_specs=pl.BlockSpec((tm, width), lambda i: (i, 0)),
        out_shape=jax.ShapeDtypeStruct((n, width), out_dtype),
        compiler_params=_cparams(1), name="proj_matmul",
    )(x, w)


def _diff_attn_kernel(slopes_ref, lamp_ref, q_ref, k_ref, v_ref, g_ref, o_ref, *,
                      q_start, n_valid, tq, tk, lam_init):
    h = pl.program_id(1)
    qi = pl.program_id(2)
    slope = slopes_ref[h]
    lamp = lamp_ref[...]
    lam = (jnp.exp(jnp.sum(lamp[0:1] * lamp[1:2], axis=1, keepdims=True))
           - jnp.exp(jnp.sum(lamp[2:3] * lamp[3:4], axis=1, keepdims=True)) + lam_init)

    q = q_ref[0] * (DIFF_QK_DIM ** -0.5)
    lane = lax.broadcasted_iota(I32, (tq, HEAD_W), 1)
    q_halves = (jnp.where(lane < DIFF_QK_DIM, q, 0.0).astype(BF16),
                jnp.where(lane >= DIFF_QK_DIM, q, 0.0).astype(BF16))

    q_pos0 = q_start + qi * tq
    last_q = q_pos0 + tq - 1
    k_end = jnp.minimum((last_q // CHUNK + 1) * CHUNK, n_valid)
    n_kb = (k_end + tk - 1) // tk
    row = lax.broadcasted_iota(I32, (tq, tk), 0) + q_pos0
    col0 = lax.broadcasted_iota(I32, (tq, tk), 1)

    def body(kj, carry):
        koff = pl.multiple_of(kj * tk, tk)
        kb = k_ref[0, pl.ds(koff, tk), :].astype(BF16)
        vb = v_ref[0, pl.ds(koff, tk), :].astype(BF16)
        col = col0 + koff
        mask = ((col >> CHUNK_SHIFT) <= (row >> CHUNK_SHIFT)) & (col < n_valid)
        bias = -slope * jnp.abs((row - col).astype(F32))
        new = []
        for c in range(2):
            m, l, acc = carry[3 * c:3 * c + 3]
            s = lax.dot_general(q_halves[c], kb, (((1,), (1,)), ((), ())),
                                preferred_element_type=F32)
            s = jnp.where(mask, s + bias, -jnp.inf)
            m_new = jnp.maximum(m, jnp.max(s, axis=1, keepdims=True))
            p = jnp.exp(s - m_new)
            alpha = jnp.exp(m - m_new)
            l = alpha * l + jnp.sum(p, axis=1, keepdims=True)
            acc = alpha * acc + jnp.dot(p.astype(BF16), vb, preferred_element_type=F32)
            new += [m_new, l, acc]
        return tuple(new)

    init = (jnp.full((tq, 1), -jnp.inf, F32), jnp.zeros((tq, 1), F32), jnp.zeros((tq, HEAD_W), F32)) * 2
    m0, l0, a0, m1, l1, a1 = lax.fori_loop(0, n_kb, body, init)
    o = a0 / l0 - lam * (a1 / l1)
    o = o * lax.rsqrt(jnp.mean(o * o, axis=1, keepdims=True) + SUBLN_EPS) * g_ref[...] * (1.0 - lam_init)
    o_ref[0] = o.astype(o_ref.dtype)


def _diff_attention(q, k, v, slopes, lamp, subln_g, *, q_start, n_valid, tq, tk, lam_init):
    b, t, _ = q.shape
    lk = k.shape[1]
    kern = functools.partial(_diff_attn_kernel, q_start=q_start, n_valid=n_valid, tq=tq, tk=tk,
                             lam_init=lam_init)
    return pl.pallas_call(
        kern,
        grid=(b, N_HEADS, t // tq),
        in_specs=[pl.BlockSpec(memory_space=pltpu.SMEM),
                  pl.BlockSpec((4, DIFF_QK_DIM), lambda bi, hi, qi: (0, 0)),
                  pl.BlockSpec((1, tq, HEAD_W), lambda bi, hi, qi: (bi, qi, hi)),
                  pl.BlockSpec((1, lk, HEAD_W), lambda bi, hi, qi: (bi, 0, hi)),
                  pl.BlockSpec((1, lk, HEAD_W), lambda bi, hi, qi: (bi, 0, hi)),
                  pl.BlockSpec((1, HEAD_W), lambda bi, hi, qi: (0, 0))],
        out_specs=pl.BlockSpec((1, tq, HEAD_W), lambda bi, hi, qi: (bi, qi, hi)),
        out_shape=jax.ShapeDtypeStruct((b, t, GROUP_W), BF16),
        compiler_params=_cparams(3), name="diff_attention",
    )(slopes, lamp, q, k, v, subln_g)


def _sb_attn_kernel(q_ref, k_ref, v_ref, o_ref, *, q_start, tq, tk):
    qi = pl.program_id(2)
    q = (q_ref[0] * (SB_HEAD_DIM ** -0.5)).astype(BF16)
    q_pos0 = q_start + qi * tq
    last_q = q_pos0 + tq - 1
    n_kb = (last_q + tk - 1) // tk
    row = lax.broadcasted_iota(I32, (tq, tk), 0) + q_pos0
    col0 = lax.broadcasted_iota(I32, (tq, tk), 1)
    tri = (lax.broadcasted_iota(I32, (tk, tk), 0) > lax.broadcasted_iota(I32, (tk, tk), 1)).astype(BF16)

    def body(i, carry):
        later, acc = carry
        kj = n_kb - 1 - i
        koff = pl.multiple_of(kj * tk, tk)
        kb = k_ref[0, pl.ds(koff, tk), :].astype(BF16)
        vb = v_ref[0, pl.ds(koff, tk), :].astype(BF16)
        mask = (col0 + koff) < row
        z = lax.dot_general(q, kb, (((1,), (1,)), ((), ())), preferred_element_type=F32)
        log_beta = jnp.minimum(z, 0.0) - jnp.log1p(jnp.exp(-jnp.abs(z)))
        log_keep = jnp.where(mask, log_beta - z, 0.0)
        hi = log_keep.astype(BF16)
        lo = (log_keep - hi.astype(F32)).astype(BF16)
        in_block = (jnp.dot(hi, tri, preferred_element_type=F32)
                    + jnp.dot(lo, tri, preferred_element_type=F32))
        a = jnp.where(mask, jnp.exp(log_beta + in_block + later), 0.0)
        acc = acc + jnp.dot(a.astype(BF16), vb, preferred_element_type=F32)
        later = later + jnp.sum(log_keep, axis=1, keepdims=True)
        return later, acc

    _, acc = lax.fori_loop(0, n_kb, body, (jnp.zeros((tq, 1), F32), jnp.zeros((tq, HEAD_W), F32)))
    o_ref[0] = acc.astype(o_ref.dtype)


def _sb_attention(q, k, v, *, q_start, tq, tk):
    b, t, _ = q.shape
    lk = k.shape[1]
    kern = functools.partial(_sb_attn_kernel, q_start=q_start, tq=tq, tk=tk)
    return pl.pallas_call(
        kern,
        grid=(b, N_HEADS, t // tq),
        in_specs=[pl.BlockSpec((1, tq, HEAD_W), lambda bi, hi, qi: (bi, qi, hi)),
                  pl.BlockSpec((1, lk, HEAD_W), lambda bi, hi, qi: (bi, 0, hi)),
                  pl.BlockSpec((1, lk, HEAD_W), lambda bi, hi, qi: (bi, 0, hi))],
        out_specs=pl.BlockSpec((1, tq, HEAD_W), lambda bi, hi, qi: (bi, qi, hi)),
        out_shape=jax.ShapeDtypeStruct((b, t, GROUP_W), BF16),
        compiler_params=_cparams(3), name="sb_attention",
    )(q, k, v)


def _layer_norm(y, g, b):
    mu = jnp.mean(y, axis=-1, keepdims=True)
    d = y - mu
    var = jnp.mean(d * d, axis=-1, keepdims=True)
    return d * lax.rsqrt(var + LN_EPS) * g + b


def _outproj_ln_kernel(x_ref, od_ref, os_ref, w_ref, g_ref, b_ref, o_ref):
    y = (DEEPNORM_ALPHA * x_ref[...]
         + jnp.dot(od_ref[...], w_ref[0:GROUP_W, :], preferred_element_type=F32)
         + jnp.dot(os_ref[...], w_ref[GROUP_W:2 * GROUP_W, :], preferred_element_type=F32))
    o_ref[...] = _layer_norm(y, g_ref[...], b_ref[...])


def _outproj_ln(x, o_diff, o_sb, w_out, g, b, tm):
    n = x.shape[0]
    tm = min(tm, n)
    return pl.pallas_call(
        _outproj_ln_kernel,
        grid=(n // tm,),
        in_specs=[pl.BlockSpec((tm, D_MODEL), lambda i: (i, 0)),
                  pl.BlockSpec((tm, GROUP_W), lambda i: (i, 0)),
                  pl.BlockSpec((tm, GROUP_W), lambda i: (i, 0)),
                  pl.BlockSpec((2 * GROUP_W, D_MODEL), lambda i: (0, 0)),
                  pl.BlockSpec((1, D_MODEL), lambda i: (0, 0)),
                  pl.BlockSpec((1, D_MODEL), lambda i: (0, 0))],
        out_specs=pl.BlockSpec((tm, D_MODEL), lambda i: (i, 0)),
        out_shape=jax.ShapeDtypeStruct((n, D_MODEL), F32),
        compiler_params=_cparams(1), name="outproj_ln",
    )(x, o_diff, o_sb, w_out, g, b)


def _extract_topk(s, vals_ref, idx_ref, base):
    n_rows = s.shape[0]
    riota = lax.broadcasted_iota(I32, s.shape, 0).astype(F32)

    def body(it, s):
        m = jnp.max(s, axis=0, keepdims=True)
        idx = jnp.min(jnp.where(s == m, riota, float(n_rows)), axis=0, keepdims=True)
        vals_ref[pl.ds(base + it, 1), :] = m
        idx_ref[pl.ds(base + it, 1), :] = idx
        return jnp.where(riota == idx, -jnp.inf, s)

    lax.fori_loop(0, PEER_TOPK, body, s)


def _route_kernel(q_ref, sub_ref, eidx_ref, gate_ref, hv_ref, hi_ref, cv_ref, ci_ref, tv_ref, te_ref):
    tm = q_ref.shape[0]
    k16 = lax.broadcasted_iota(I32, (PEER_TOPK, tm), 0).astype(F32)
    for h in range(PEER_HEADS):
        for c in range(2):
            hc = 2 * h + c
            s = lax.dot_general(sub_ref[hc], q_ref[:, hc * PEER_N_KEYS:(hc + 1) * PEER_N_KEYS],
                                (((1,), (1,)), ((), ())), preferred_element_type=F32)
            _extract_topk(s, hv_ref, hi_ref, c * PEER_TOPK)
        s1, s2 = hv_ref[0:PEER_TOPK, :], hv_ref[PEER_TOPK:2 * PEER_TOPK, :]
        i1, i2 = hi_ref[0:PEER_TOPK, :], hi_ref[PEER_TOPK:2 * PEER_TOPK, :]
        cand = jnp.concatenate([s1[a:a + 1, :] + s2 for a in range(PEER_TOPK)], axis=0)
        _extract_topk(cand, cv_ref, ci_ref, 0)
        pos = ci_ref[...]
        top = cv_ref[...]
        rows = []
        for j in range(PEER_TOPK):
            pa = jnp.floor(pos[j:j + 1, :] * (1.0 / PEER_TOPK))
            pb = pos[j:j + 1, :] - pa * PEER_TOPK
            ea = jnp.sum(jnp.where(k16 == pa, i1, 0.0), axis=0, keepdims=True)
            eb = jnp.sum(jnp.where(k16 == pb, i2, 0.0), axis=0, keepdims=True)
            rows.append(ea * PEER_N_KEYS + eb)
        e = jnp.exp(top - top[0:1, :])
        tv_ref[h * PEER_TOPK:(h + 1) * PEER_TOPK, :] = e / jnp.sum(e, axis=0, keepdims=True)
        te_ref[h * PEER_TOPK:(h + 1) * PEER_TOPK, :] = jnp.concatenate(rows, axis=0) * ROW_TILES
    gate_ref[...] = tv_ref[...].T
    eidx_ref[...] = te_ref[...].T.astype(I32)


def _peer_route(qp, subkeys, tm):
    n = qp.shape[0]
    return pl.pallas_call(
        _route_kernel,
        grid=(n // tm,),
        in_specs=[pl.BlockSpec((tm, PEER_HEADS * 2 * PEER_N_KEYS), lambda i: (i, 0)),
                  pl.BlockSpec((2 * PEER_HEADS, PEER_N_KEYS, PEER_N_KEYS), lambda i: (0, 0, 0))],
        out_specs=[pl.BlockSpec((tm, PEER_PICKS), lambda i: (i, 0)),
                   pl.BlockSpec((tm, PEER_PICKS), lambda i: (i, 0))],
        out_shape=[jax.ShapeDtypeStruct((n, PEER_PICKS), I32),
                   jax.ShapeDtypeStruct((n, PEER_PICKS), F32)],
        scratch_shapes=[pltpu.VMEM((2 * PEER_TOPK, tm), F32), pltpu.VMEM((2 * PEER_TOPK, tm), F32),
                        pltpu.VMEM((PEER_TOPK, tm), F32), pltpu.VMEM((PEER_TOPK, tm), F32),
                        pltpu.VMEM((PEER_PICKS, tm), F32), pltpu.VMEM((PEER_PICKS, tm), F32)],
        compiler_params=_cparams(1), name="peer_route",
    )(qp, subkeys)


N_SLOTS = 4
ISSUE_PER_STEP = PEER_PICKS // ROW_TILES


def _expert_kernel(eidx_ref, gate_ref, x_ref, tab_ref, g_ref, b_ref, o_ref, buf_ref, po_ref, sem_ref):
    tb = x_ref.shape[0]
    depth = N_SLOTS - 1

    def row_copy(tok, r, slot):
        off = pl.multiple_of(eidx_ref[tok, r], ROW_TILES)
        return pltpu.make_async_copy(tab_ref.at[pl.ds(off, ROW_TILES), :],
                                     buf_ref.at[slot, pl.ds(r * ROW_PITCH, ROW_TILES), :],
                                     sem_ref.at[slot])

    def wait_slot(slot):
        pltpu.make_async_copy(tab_ref.at[pl.ds(0, PEER_PICKS * ROW_TILES), :],
                              buf_ref.at[slot, pl.ds(0, PEER_PICKS * ROW_TILES), :],
                              sem_ref.at[slot]).wait()

    for tok in range(depth):
        for r in range(PEER_PICKS):
            row_copy(tok, r, tok).start()

    s16 = lax.broadcasted_iota(I32, (D_TILES, LANES), 0)

    def group(gi, _):
        for slot in range(N_SLOTS):
            tok = gi * N_SLOTS + slot
            nxt = jnp.minimum(tok + depth, tb - 1)
            nslot = (slot + depth) % N_SLOTS
            wait_slot(slot)
            xt = x_ref[tok]
            step = 0
            acc = jnp.zeros((D_TILES, PEER_PICKS), F32)
            for s in range(D_TILES):
                u_s = buf_ref[slot, pl.ds(s, PEER_PICKS, stride=ROW_PITCH), :].astype(BF16)
                x_s = jnp.where(s16 == s, xt, 0.0).astype(BF16)
                acc = acc + lax.dot_general(x_s, u_s, (((1,), (1,)), ((), ())), preferred_element_type=F32)
                for r in range(step * ISSUE_PER_STEP, (step + 1) * ISSUE_PER_STEP):
                    row_copy(nxt, r, nslot).start()
                step += 1
            pre = jnp.sum(acc, axis=0, keepdims=True)
            act = 0.5 * pre * (1.0 + lax.erf(pre * (2.0 ** -0.5))) * gate_ref[pl.ds(tok, 1), :]
            act = jnp.broadcast_to(act, (SUBLANES, PEER_PICKS)).astype(BF16)
            for s in range(D_TILES):
                v_s = buf_ref[slot, pl.ds(D_TILES + s, PEER_PICKS, stride=ROW_PITCH), :].astype(BF16)
                o_s = jnp.dot(act, v_s, preferred_element_type=F32)
                po_ref[tok, pl.ds(s, 1), :] = o_s[0:1, :]
                for r in range(step * ISSUE_PER_STEP, (step + 1) * ISSUE_PER_STEP):
                    row_copy(nxt, r, nslot).start()
                step += 1
        return 0

    lax.fori_loop(0, tb // N_SLOTS, group, 0)
    for i in range(depth):
        wait_slot((tb + i) % N_SLOTS)

    y = DEEPNORM_ALPHA * x_ref[...] + po_ref[...]
    inv_d = 1.0 / D_MODEL
    mu = jnp.sum(jnp.sum(y, axis=2, keepdims=True), axis=1, keepdims=True) * inv_d
    d = y - mu
    var = jnp.sum(jnp.sum(d * d, axis=2, keepdims=True), axis=1, keepdims=True) * inv_d
    o_ref[...] = d * lax.rsqrt(var + LN_EPS) * g_ref[...] + b_ref[...]


def _peer_experts(eidx, gates, x3, table, g3, b3, tb):
    n = x3.shape[0]
    tok_spec = pl.BlockSpec((tb, D_TILES, LANES), lambda i: (i, 0, 0))
    par_spec = pl.BlockSpec((1, D_TILES, LANES), lambda i: (0, 0, 0))
    return pl.pallas_call(
        _expert_kernel,
        grid=(n // tb,),
        in_specs=[pl.BlockSpec((tb, PEER_PICKS), lambda i: (i, 0), memory_space=pltpu.SMEM),
                  pl.BlockSpec((tb, PEER_PICKS), lambda i: (i, 0)),
                  tok_spec,
                  pl.BlockSpec(memory_space=pl.ANY),
                  par_spec, par_spec],
        out_specs=tok_spec,
        out_shape=jax.ShapeDtypeStruct((n, D_TILES, LANES), F32),
        scratch_shapes=[pltpu.VMEM((N_SLOTS, PEER_PICKS * ROW_PITCH, LANES), F32),
                        pltpu.VMEM((tb, D_TILES, LANES), F32),
                        pltpu.SemaphoreType.DMA((N_SLOTS,))],
        compiler_params=_cparams(1), name="peer_experts",
    )(eidx, gates, x3, table, g3, b3)


def _pad_keys(past, new, tk):
    cat = jnp.concatenate([past, new], axis=1)
    pad = (-cat.shape[1]) % tk
    return jnp.pad(cat, ((0, 0), (0, pad), (0, 0)))


def _layer(x, past, lam_init, p):
    b, t, _ = x.shape
    n = b * t
    xf = x.reshape(n, D_MODEL)
    tm = 512
    qd, kd, vd, qs, ks, vs = [_matmul(xf, p["w_in"], j, GROUP_W, F32, tm).reshape(b, t, GROUP_W)
                              for j in range(6)]
    if past is None:
        q_start, tq, tk = 0, 128, 128
        keys = (kd, vd, ks, vs)
    else:
        q_start, tq, tk = past[0].shape[1], t, 128
        keys = tuple(_pad_keys(c.reshape(b, q_start, GROUP_W), r, tk) for c, r in zip(past, (kd, vd, ks, vs)))
    o_diff = _diff_attention(qd, keys[0], keys[1], p["slopes"], p["lamp"], p["subln_g"],
                             q_start=q_start, n_valid=q_start + t, tq=tq, tk=tk, lam_init=lam_init)
    o_sb = _sb_attention(qs, keys[2], keys[3], q_start=q_start, tq=tq, tk=tk)
    x1 = _outproj_ln(xf, o_diff.reshape(n, GROUP_W), o_sb.reshape(n, GROUP_W), p["w_out"],
                     p["ln1_g"], p["ln1_b"], 256)
    qp = _matmul(x1, p["peer_w_q"], 0, PEER_HEADS * 2 * PEER_N_KEYS, BF16, tm)
    eidx, gates = _peer_route(qp, p["subkeys"], 128)
    y3 = _peer_experts(eidx, gates, x1.reshape(n, D_TILES, LANES), p["table"], p["ln2_g"], p["ln2_b"], 128)
    return y3.reshape(b, t, D_MODEL), (kd, vd, ks, vs)


def kernel(x_prompt, x_sample, cache_diff_k, cache_diff_v, cache_sb_k, cache_sb_v, w_in, w_out, lam_q1, lam_k1, lam_q2, lam_k2, subln_g, ln1_g, ln1_b, ln2_g, ln2_b, peer_w_q, peer_subkeys, peer_u, peer_v):
    depth = w_in.shape[0]
    n_exp = peer_u.shape[1]
    slopes = jnp.exp2(-8.0 * (jnp.arange(N_HEADS, dtype=F32) + 1.0) / N_HEADS)
    y_p, y_s = x_prompt, x_sample
    rows_p, rows_s = [], []
    for l in range(depth):
        lam_init = 0.8 - 0.6 * math.exp(-0.3 * l)
        p = {
            "w_in": w_in[l].astype(BF16),
            "w_out": w_out[l].astype(BF16),
            "slopes": slopes,
            "lamp": jnp.stack([lam_q1[l], lam_k1[l], lam_q2[l], lam_k2[l]]),
            "subln_g": subln_g[l].reshape(1, HEAD_W),
            "ln1_g": ln1_g[l].reshape(1, D_MODEL), "ln1_b": ln1_b[l].reshape(1, D_MODEL),
            "ln2_g": ln2_g[l].reshape(1, D_TILES, LANES), "ln2_b": ln2_b[l].reshape(1, D_TILES, LANES),
            "peer_w_q": peer_w_q[l].astype(BF16),
            "subkeys": peer_subkeys[l].reshape(2 * PEER_HEADS, PEER_N_KEYS, PEER_N_KEYS).astype(BF16),
            "table": jnp.concatenate([peer_u[l].reshape(n_exp, D_TILES, LANES),
                                      peer_v[l].reshape(n_exp, D_TILES, LANES)],
                                     axis=1).reshape(n_exp * ROW_TILES, LANES),
        }
        y_p, rp = _layer(y_p, None, lam_init, p)
        past = (cache_diff_k[l], cache_diff_v[l], cache_sb_k[l], cache_sb_v[l])
        y_s, rs = _layer(y_s, past, lam_init, p)
        rows_p.append(rp)
        rows_s.append(rs)

    def stack(rows, i, tail):
        return jnp.stack([r[i].reshape(r[i].shape[:2] + tail) for r in rows])

    kshape = (N_HEADS, 2, DIFF_QK_DIM)
    vshape = (N_HEADS, HEAD_W)
    return (y_p, y_s,
            stack(rows_p, 0, kshape), stack(rows_p, 1, vshape), stack(rows_p, 2, vshape), stack(rows_p, 3, vshape),
            stack(rows_s, 0, kshape), stack(rows_s, 1, vshape), stack(rows_s, 2, vshape), stack(rows_s, 3, vshape))
```

```python
import functools
import math

import numpy as np
import jax
import jax.numpy as jnp
from jax import lax
from jax.experimental import pallas as pl
from jax.experimental.pallas import tpu as pltpu

F32 = jnp.float32
BF16 = jnp.bfloat16
I32 = jnp.int32

D_MODEL = 2048
CHUNK = 64
CHUNK_SHIFT = 6
N_HEADS = 8
HEAD_W = 128
DIFF_QK_DIM = 64
SB_HEAD_DIM = 128
GROUP_W = N_HEADS * HEAD_W
PEER_HEADS = 8
PEER_N_KEYS = 128
PEER_TOPK = 16
PEER_PICKS = PEER_HEADS * PEER_TOPK
DEPTH = 1
DEEPNORM_ALPHA = (2.0 * DEPTH) ** 0.25
LN_EPS = 1e-5
SUBLN_EPS = 1e-5

LANES = 128
SUBLANES = 8
D_TILES = D_MODEL // LANES
ROW_TILES = 2 * D_TILES
ROW_PITCH = ROW_TILES + 4
VMEM_LIMIT = 56 * 1024 * 1024


def _cparams(n_axes):
    return pltpu.CompilerParams(dimension_semantics=("arbitrary",) * n_axes,
                                vmem_limit_bytes=VMEM_LIMIT)


def _dot_nt(a, b):
    return lax.dot_general(a, b, (((1,), (1,)), ((), ())), preferred_element_type=F32)


def _mm_kernel(x_ref, w_ref, o_ref):
    o_ref[...] = jnp.dot(x_ref[...].astype(BF16), w_ref[...],
                         preferred_element_type=F32).astype(o_ref.dtype)


def _matmul(x, w, col_block, width, out_dtype, tm):
    n, k = x.shape
    tm = min(tm, n)
    return pl.pallas_call(
        _mm_kernel,
        grid=(n // tm,),
        in_specs=[pl.BlockSpec((tm, k), lambda i: (i, 0)),
                  pl.BlockSpec((k, width), lambda i: (0, col_block))],
        out_specs=pl.BlockSpec((tm, width), lambda i: (i, 0)),
        out_shape=jax.ShapeDtypeStruct((n, width), out_dtype),
        compiler_params=_cparams(1), name="proj_matmul",
    )(x, w)


def _diff_attn_kernel(slopes_ref, lamp_ref, q_ref, k_ref, v_ref, g_ref, o_ref, *,
                      q_start, n_valid, tq, tk, lam_init):
    h = pl.program_id(1)
    qi = pl.program_id(2)
    slope = slopes_ref[h]
    lamp = lamp_ref[...]
    lam = (jnp.exp(jnp.sum(lamp[0:1] * lamp[1:2], axis=1, keepdims=True))
           - jnp.exp(jnp.sum(lamp[2:3] * lamp[3:4], axis=1, keepdims=True)) + lam_init)

    q = q_ref[0] * (DIFF_QK_DIM ** -0.5)
    lane = lax.broadcasted_iota(I32, (tq, HEAD_W), 1)
    qq = jnp.concatenate([jnp.where(lane < DIFF_QK_DIM, q, 0.0),
                          jnp.where(lane >= DIFF_QK_DIM, q, 0.0)], axis=0).astype(BF16)

    q_pos0 = q_start + qi * tq
    last_q = q_pos0 + tq - 1
    k_end = jnp.minimum((last_q // CHUNK + 1) * CHUNK, n_valid)
    n_kb = (k_end + tk - 1) // tk
    n_before = q_pos0 // tk
    col1 = lax.broadcasted_iota(I32, (1, tk), 1)

    def update(carry, s, vb):
        m, l, acc = carry
        m_new = jnp.maximum(m, jnp.max(s, axis=1, keepdims=True))
        p = jnp.exp(s - m_new)
        alpha = jnp.exp(m - m_new)
        l = alpha * l + jnp.sum(p, axis=1, keepdims=True)
        acc = alpha * acc + jnp.dot(p.astype(BF16), vb, preferred_element_type=F32)
        return m_new, l, acc

    def load(kj):
        koff = pl.multiple_of(kj * tk, tk)
        return (koff, k_ref[0, pl.ds(koff, tk), :].astype(BF16), v_ref[0, pl.ds(koff, tk), :].astype(BF16))

    def before_body(kj, carry):
        koff, kb, vb = load(kj)
        s = _dot_nt(qq, kb) + slope * (col1 + (koff - q_pos0)).astype(F32)
        return update(carry, s, vb)

    def edge_body(kj, carry):
        koff, kb, vb = load(kj)
        r = lax.broadcasted_iota(I32, (2 * tq, tk), 0)
        rel = jnp.where(r >= tq, r - tq, r)
        row = rel + q_pos0
        col = lax.broadcasted_iota(I32, (2 * tq, tk), 1) + koff
        mask = ((col >> CHUNK_SHIFT) <= (row >> CHUNK_SHIFT)) & (col < n_valid)
        bias = slope * (rel.astype(F32) - jnp.abs((row - col).astype(F32)))
        s = jnp.where(mask, _dot_nt(qq, kb) + bias, -jnp.inf)
        return update(carry, s, vb)

    init = (jnp.full((2 * tq, 1), -jnp.inf, F32), jnp.zeros((2 * tq, 1), F32), jnp.zeros((2 * tq, HEAD_W), F32))
    carry = lax.fori_loop(0, n_before, before_body, init)
    _, l, acc = lax.fori_loop(n_before, n_kb, edge_body, carry)
    o = acc / l
    o = o[0:tq] - lam * o[tq:2 * tq]
    o = o * lax.rsqrt(jnp.mean(o * o, axis=1, keepdims=True) + SUBLN_EPS) * g_ref[...] * (1.0 - lam_init)
    o_ref[0] = o.astype(o_ref.dtype)


def _diff_attention(q, k, v, slopes, lamp, subln_g, *, q_start, n_valid, tq, tk, lam_init):
    b, t, _ = q.shape
    lk = k.shape[1]
    kern = functools.partial(_diff_attn_kernel, q_start=q_start, n_valid=n_valid, tq=tq, tk=tk,
                             lam_init=lam_init)
    return pl.pallas_call(
        kern,
        grid=(b, N_HEADS, t // tq),
        in_specs=[pl.BlockSpec(memory_space=pltpu.SMEM),
                  pl.BlockSpec((4, DIFF_QK_DIM), lambda bi, hi, qi: (0, 0)),
                  pl.BlockSpec((1, tq, HEAD_W), lambda bi, hi, qi: (bi, qi, hi)),
                  pl.BlockSpec((1, lk, HEAD_W), lambda bi, hi, qi: (bi, 0, hi)),
                  pl.BlockSpec((1, lk, HEAD_W), lambda bi, hi, qi: (bi, 0, hi)),
                  pl.BlockSpec((1, HEAD_W), lambda bi, hi, qi: (0, 0))],
        out_specs=pl.BlockSpec((1, tq, HEAD_W), lambda bi, hi, qi: (bi, qi, hi)),
        out_shape=jax.ShapeDtypeStruct((b, t, GROUP_W), BF16),
        compiler_params=_cparams(3), name="diff_attention",
    )(slopes, lamp, q, k, v, subln_g)


def _sb_attn_kernel(q_ref, k_ref, v_ref, o_ref, *, q_start, tq, tk):
    qi = pl.program_id(2)
    q = (q_ref[0] * (SB_HEAD_DIM ** -0.5)).astype(BF16)
    q_pos0 = q_start + qi * tq
    last_q = q_pos0 + tq - 1
    n_kb = (last_q + tk - 1) // tk
    n_before = q_pos0 // tk
    tri = (lax.broadcasted_iota(I32, (tk, tk), 0) > lax.broadcasted_iota(I32, (tk, tk), 1)).astype(BF16)

    def step(kj, carry, masked):
        later, acc = carry
        koff = pl.multiple_of(kj * tk, tk)
        kb = k_ref[0, pl.ds(koff, tk), :].astype(BF16)
        vb = v_ref[0, pl.ds(koff, tk), :].astype(BF16)
        z = _dot_nt(q, kb)
        log_beta = jnp.minimum(z, 0.0) - jnp.log1p(jnp.exp(-jnp.abs(z)))
        log_keep = log_beta - z
        if masked:
            row = lax.broadcasted_iota(I32, (tq, tk), 0) + q_pos0
            mask = (lax.broadcasted_iota(I32, (tq, tk), 1) + koff) < row
            log_keep = jnp.where(mask, log_keep, 0.0)
        hi = log_keep.astype(BF16)
        lo = (log_keep - hi.astype(F32)).astype(BF16)
        in_block = (jnp.dot(hi, tri, preferred_element_type=F32)
                    + jnp.dot(lo, tri, preferred_element_type=F32))
        a = jnp.exp(log_beta + in_block + later)
        if masked:
            a = jnp.where(mask, a, 0.0)
        acc = acc + jnp.dot(a.astype(BF16), vb, preferred_element_type=F32)
        later = later + jnp.sum(log_keep, axis=1, keepdims=True)
        return later, acc

    init = (jnp.zeros((tq, 1), F32), jnp.zeros((tq, HEAD_W), F32))
    carry = lax.fori_loop(0, n_kb - n_before, lambda i, c: step(n_kb - 1 - i, c, True), init)
    _, acc = lax.fori_loop(0, n_before, lambda i, c: step(n_before - 1 - i, c, False), carry)
    o_ref[0] = acc.astype(o_ref.dtype)


def _sb_attention(q, k, v, *, q_start, tq, tk):
    b, t, _ = q.shape
    lk = k.shape[1]
    kern = functools.partial(_sb_attn_kernel, q_start=q_start, tq=tq, tk=tk)
    return pl.pallas_call(
        kern,
        grid=(b, N_HEADS, t // tq),
        in_specs=[pl.BlockSpec((1, tq, HEAD_W), lambda bi, hi, qi: (bi, qi, hi)),
                  pl.BlockSpec((1, lk, HEAD_W), lambda bi, hi, qi: (bi, 0, hi)),
                  pl.BlockSpec((1, lk, HEAD_W), lambda bi, hi, qi: (bi, 0, hi))],
        out_specs=pl.BlockSpec((1, tq, HEAD_W), lambda bi, hi, qi: (bi, qi, hi)),
        out_shape=jax.ShapeDtypeStruct((b, t, GROUP_W), BF16),
        compiler_params=_cparams(3), name="sb_attention",
    )(q, k, v)


def _layer_norm(y, g, b):
    mu = jnp.mean(y, axis=-1, keepdims=True)
    d = y - mu
    var = jnp.mean(d * d, axis=-1, keepdims=True)
    return d * lax.rsqrt(var + LN_EPS) * g + b


def _outproj_ln_kernel(x_ref, od_ref, os_ref, w_ref, g_ref, b_ref, o_ref):
    y = (DEEPNORM_ALPHA * x_ref[...]
         + jnp.dot(od_ref[...], w_ref[0:GROUP_W, :], preferred_element_type=F32)
         + jnp.dot(os_ref[...], w_ref[GROUP_W:2 * GROUP_W, :], preferred_element_type=F32))
    o_ref[...] = _layer_norm(y, g_ref[...], b_ref[...])


def _outproj_ln(x, o_diff, o_sb, w_out, g, b, tm):
    n = x.shape[0]
    tm = min(tm, n)
    return pl.pallas_call(
        _outproj_ln_kernel,
        grid=(n // tm,),
        in_specs=[pl.BlockSpec((tm, D_MODEL), lambda i: (i, 0)),
                  pl.BlockSpec((tm, GROUP_W), lambda i: (i, 0)),
                  pl.BlockSpec((tm, GROUP_W), lambda i: (i, 0)),
                  pl.BlockSpec((2 * GROUP_W, D_MODEL), lambda i: (0, 0)),
                  pl.BlockSpec((1, D_MODEL), lambda i: (0, 0)),
                  pl.BlockSpec((1, D_MODEL), lambda i: (0, 0))],
        out_specs=pl.BlockSpec((tm, D_MODEL), lambda i: (i, 0)),
        out_shape=jax.ShapeDtypeStruct((n, D_MODEL), F32),
        compiler_params=_cparams(1), name="outproj_ln",
    )(x, o_diff, o_sb, w_out, g, b)


_CAND_NB = [PEER_TOPK // (a + 1) for a in range(PEER_TOPK)]
_CAND_ROWS = sum(_CAND_NB)
_CAND_PAD = -(-_CAND_ROWS // SUBLANES) * SUBLANES
_NO_CAND = float(PEER_TOPK * PEER_TOPK)
ROUTE_HEAD_GROUP = 4


def _cand_flat_index(tm):
    flat = [a * PEER_TOPK + b for a in range(PEER_TOPK) for b in range(_CAND_NB[a])]
    flat += [_NO_CAND] * (_CAND_PAD - _CAND_ROWS)
    return jnp.asarray(np.broadcast_to(np.asarray(flat, np.float32)[:, None], (_CAND_PAD, tm)))


def _extract_topk(scores, ids, vals_ref, idx_ref, bases, no_id):
    def body(it, carry):
        out = []
        for s, rid, base in zip(carry, ids, bases):
            m = jnp.max(s, axis=0, keepdims=True)
            idx = jnp.min(jnp.where(s == m, rid, no_id), axis=0, keepdims=True)
            vals_ref[pl.ds(base + it, 1), :] = m
            idx_ref[pl.ds(base + it, 1), :] = idx
            out.append(jnp.where(rid == idx, -jnp.inf, s))
        return tuple(out)

    lax.fori_loop(0, PEER_TOPK, body, tuple(scores))


def _route_kernel(q_ref, sub_ref, flat_ref, eidx_ref, gate_ref, hv_ref, hi_ref, cand_ref, cv_ref, ci_ref,
                  tv_ref, te_ref):
    tm = q_ref.shape[0]
    key_id = lax.broadcasted_iota(I32, (PEER_N_KEYS, tm), 0).astype(F32)
    k16 = lax.broadcasted_iota(I32, (PEER_TOPK, tm), 0).astype(F32)
    two_k = 2 * PEER_TOPK
    for h in range(PEER_HEADS):
        scores = [_dot_nt(sub_ref[2 * h + c], q_ref[:, (2 * h + c) * PEER_N_KEYS:(2 * h + c + 1) * PEER_N_KEYS])
                  for c in range(2)]
        _extract_topk(scores, [key_id, key_id], hv_ref, hi_ref,
                      [h * two_k, h * two_k + PEER_TOPK], float(PEER_N_KEYS))
    flat = flat_ref[...]
    for h0 in range(0, PEER_HEADS, ROUTE_HEAD_GROUP):
        cands = []
        for g in range(ROUTE_HEAD_GROUP):
            base = (h0 + g) * two_k
            cand_ref[g, _CAND_PAD - SUBLANES:_CAND_PAD, :] = jnp.full((SUBLANES, tm), -jnp.inf, F32)
            off = 0
            for a in range(PEER_TOPK):
                nb = _CAND_NB[a]
                cand_ref[g, off:off + nb, :] = (hv_ref[base + a:base + a + 1, :]
                                                + hv_ref[base + PEER_TOPK:base + PEER_TOPK + nb, :])
                off += nb
            cands.append(cand_ref[g])
        _extract_topk(cands, [flat] * ROUTE_HEAD_GROUP, cv_ref, ci_ref,
                      [(h0 + g) * PEER_TOPK for g in range(ROUTE_HEAD_GROUP)], _NO_CAND)
    for h in range(PEER_HEADS):
        base = h * two_k
        i1, i2 = hi_ref[base:base + PEER_TOPK, :], hi_ref[base + PEER_TOPK:base + two_k, :]
        pos = ci_ref[h * PEER_TOPK:(h + 1) * PEER_TOPK, :]
        top = cv_ref[h * PEER_TOPK:(h + 1) * PEER_TOPK, :]
        rows = []
        for j in range(PEER_TOPK):
            pa = jnp.floor(pos[j:j + 1, :] * (1.0 / PEER_TOPK))
            pb = pos[j:j + 1, :] - pa * PEER_TOPK
            ea = jnp.sum(jnp.where(k16 == pa, i1, 0.0), axis=0, keepdims=True)
            eb = jnp.sum(jnp.where(k16 == pb, i2, 0.0), axis=0, keepdims=True)
            rows.append(ea * PEER_N_KEYS + eb)
        e = jnp.exp(top - top[0:1, :])
        tv_ref[h * PEER_TOPK:(h + 1) * PEER_TOPK, :] = e / jnp.sum(e, axis=0, keepdims=True)
        te_ref[h * PEER_TOPK:(h + 1) * PEER_TOPK, :] = jnp.concatenate(rows, axis=0) * ROW_TILES
    gate_ref[...] = tv_ref[...].T
    eidx_ref[...] = te_ref[...].T.astype(I32)


def _peer_route(qp, subkeys, tm):
    n = qp.shape[0]
    return pl.pallas_call(
        _route_kernel,
        grid=(n // tm,),
        in_specs=[pl.BlockSpec((tm, PEER_HEADS * 2 * PEER_N_KEYS), lambda i: (i, 0)),
                  pl.BlockSpec((2 * PEER_HEADS, PEER_N_KEYS, PEER_N_KEYS), lambda i: (0, 0, 0)),
                  pl.BlockSpec((_CAND_PAD, tm), lambda i: (0, 0))],
        out_specs=[pl.BlockSpec((tm, PEER_PICKS), lambda i: (i, 0)),
                   pl.BlockSpec((tm, PEER_PICKS), lambda i: (i, 0))],
        out_shape=[jax.ShapeDtypeStruct((n, PEER_PICKS), I32),
                   jax.ShapeDtypeStruct((n, PEER_PICKS), F32)],
        scratch_shapes=[pltpu.VMEM((PEER_HEADS * 2 * PEER_TOPK, tm), F32),
                        pltpu.VMEM((PEER_HEADS * 2 * PEER_TOPK, tm), F32),
                        pltpu.VMEM((ROUTE_HEAD_GROUP, _CAND_PAD, tm), F32),
                        pltpu.VMEM((PEER_PICKS, tm), F32), pltpu.VMEM((PEER_PICKS, tm), F32),
                        pltpu.VMEM((PEER_PICKS, tm), F32), pltpu.VMEM((PEER_PICKS, tm), F32)],
        compiler_params=_cparams(1), name="peer_route",
    )(qp, subkeys, _cand_flat_index(tm))


N_SLOTS = 6
GATHER_AHEAD = N_SLOTS - 2
ISSUE_PER_STEP = PEER_PICKS // D_TILES


def _expert_kernel(eidx_ref, gate_ref, x_ref, tab_ref, g_ref, b_ref, o_ref, buf_ref, po_ref, sem_ref):
    tb = x_ref.shape[0]

    def row_copy(tok, r, slot):
        off = pl.multiple_of(eidx_ref[tok, r], ROW_TILES)
        return pltpu.make_async_copy(tab_ref.at[pl.ds(off, ROW_TILES), :],
                                     buf_ref.at[slot, pl.ds(r * ROW_PITCH, ROW_TILES), :],
                                     sem_ref.at[slot])

    def wait_slot(slot):
        pltpu.make_async_copy(tab_ref.at[pl.ds(0, PEER_PICKS * ROW_TILES), :],
                              buf_ref.at[slot, pl.ds(0, PEER_PICKS * ROW_TILES), :],
                              sem_ref.at[slot]).wait()

    lane = lax.broadcasted_iota(I32, (LANES, LANES), 1)
    pad_rows = jnp.zeros((LANES - D_TILES, LANES), F32)

    def x_weights(tok):
        return jnp.concatenate([x_ref[tok], pad_rows], axis=0).T

    def up_step(slot, xt, s, acc):
        u_s = buf_ref[slot, pl.ds(s, PEER_PICKS, stride=ROW_PITCH), :].astype(BF16)
        w_s = jnp.where(lane == s, xt, 0.0).astype(BF16)
        return acc + jnp.dot(u_s, w_s, preferred_element_type=F32)

    def activation(tok, acc):
        pre = jnp.sum(acc.T[0:D_TILES, :], axis=0, keepdims=True)
        act = 0.5 * pre * (1.0 + lax.erf(pre * (2.0 ** -0.5))) * gate_ref[pl.ds(tok, 1), :]
        return jnp.broadcast_to(act, (SUBLANES, PEER_PICKS)).astype(BF16)

    def down_step(slot, tok, act, s):
        v_s = buf_ref[slot, pl.ds(D_TILES + s, PEER_PICKS, stride=ROW_PITCH), :].astype(BF16)
        o_s = jnp.dot(act, v_s, preferred_element_type=F32)
        po_ref[tok, pl.ds(s, 1), :] = o_s[0:1, :]

    for tok in range(GATHER_AHEAD):
        for r in range(PEER_PICKS):
            row_copy(tok, r, tok).start()

    wait_slot(0)
    xt = x_weights(0)
    acc = jnp.zeros((PEER_PICKS, LANES), F32)
    for s in range(D_TILES):
        acc = up_step(0, xt, s, acc)
    act0 = activation(0, acc)

    def token(tok, act_prev):
        slot = lax.rem(tok, N_SLOTS)
        pslot = lax.rem(tok - 1, N_SLOTS)
        nxt = jnp.minimum(tok - 1 + GATHER_AHEAD, tb - 1)
        nslot = lax.rem(tok - 1 + GATHER_AHEAD, N_SLOTS)
        wait_slot(slot)
        xt = x_weights(tok)
        acc = jnp.zeros((PEER_PICKS, LANES), F32)
        for s in range(D_TILES):
            acc = up_step(slot, xt, s, acc)
            down_step(pslot, tok - 1, act_prev, s)
            for r in range(s * ISSUE_PER_STEP, (s + 1) * ISSUE_PER_STEP):
                row_copy(nxt, r, nslot).start()
        return activation(tok, acc)

    act_last = lax.fori_loop(1, tb, token, act0)
    for s in range(D_TILES):
        down_step((tb - 1) % N_SLOTS, tb - 1, act_last, s)
    for i in range(GATHER_AHEAD - 1):
        wait_slot((tb + i) % N_SLOTS)

    y = DEEPNORM_ALPHA * x_ref[...] + po_ref[...]
    inv_d = 1.0 / D_MODEL
    mu = jnp.sum(jnp.sum(y, axis=2, keepdims=True), axis=1, keepdims=True) * inv_d
    d = y - mu
    var = jnp.sum(jnp.sum(d * d, axis=2, keepdims=True), axis=1, keepdims=True) * inv_d
    o_ref[...] = d * lax.rsqrt(var + LN_EPS) * g_ref[...] + b_ref[...]


def _peer_experts(eidx, gates, x3, table, g3, b3, tb):
    n = x3.shape[0]
    tb = min(tb, n)
    tok_spec = pl.BlockSpec((tb, D_TILES, LANES), lambda i: (i, 0, 0))
    par_spec = pl.BlockSpec((1, D_TILES, LANES), lambda i: (0, 0, 0))
    return pl.pallas_call(
        _expert_kernel,
        grid=(n // tb,),
        in_specs=[pl.BlockSpec((tb, PEER_PICKS), lambda i: (i, 0), memory_space=pltpu.SMEM),
                  pl.BlockSpec((tb, PEER_PICKS), lambda i: (i, 0)),
                  tok_spec,
                  pl.BlockSpec(memory_space=pl.ANY),
                  par_spec, par_spec],
        out_specs=tok_spec,
        out_shape=jax.ShapeDtypeStruct((n, D_TILES, LANES), F32),
        scratch_shapes=[pltpu.VMEM((N_SLOTS, PEER_PICKS * ROW_PITCH, LANES), F32),
                        pltpu.VMEM((tb, D_TILES, LANES), F32),
                        pltpu.SemaphoreType.DMA((N_SLOTS,))],
        compiler_params=_cparams(1), name="peer_experts",
    )(eidx, gates, x3, table, g3, b3)


def _pad_keys(past, new, tk):
    cat = jnp.concatenate([past, new], axis=1)
    pad = (-cat.shape[1]) % tk
    return jnp.pad(cat, ((0, 0), (0, pad), (0, 0)))


def _layer(x, past, lam_init, p):
    b, t, _ = x.shape
    n = b * t
    xf = x.reshape(n, D_MODEL)
    tm = 512
    qd, kd, vd, qs, ks, vs = [_matmul(xf, p["w_in"], j, GROUP_W, F32, tm).reshape(b, t, GROUP_W)
                              for j in range(6)]
    tk = 256
    if past is None:
        q_start, tq = 0, tk
        keys = (kd, vd, ks, vs)
    else:
        q_start, tq = past[0].shape[1], t
        keys = tuple(_pad_keys(c.reshape(b, q_start, GROUP_W), r, tk) for c, r in zip(past, (kd, vd, ks, vs)))
    o_diff = _diff_attention(qd, keys[0], keys[1], p["slopes"], p["lamp"], p["subln_g"],
                             q_start=q_start, n_valid=q_start + t, tq=tq, tk=tk, lam_init=lam_init)
    o_sb = _sb_attention(qs, keys[2], keys[3], q_start=q_start, tq=tq, tk=tk)
    x1 = _outproj_ln(xf, o_diff.reshape(n, GROUP_W), o_sb.reshape(n, GROUP_W), p["w_out"],
                     p["ln1_g"], p["ln1_b"], 256)
    qp = _matmul(x1, p["peer_w_q"], 0, PEER_HEADS * 2 * PEER_N_KEYS, BF16, tm)
    eidx, gates = _peer_route(qp, p["subkeys"], 128)
    y3 = _peer_experts(eidx, gates, x1.reshape(n, D_TILES, LANES), p["table"], p["ln2_g"], p["ln2_b"], 128)
    return y3.reshape(b, t, D_MODEL), (kd, vd, ks, vs)


def kernel(x_prompt, x_sample, cache_diff_k, cache_diff_v, cache_sb_k, cache_sb_v, w_in, w_out, lam_q1, lam_k1, lam_q2, lam_k2, subln_g, ln1_g, ln1_b, ln2_g, ln2_b, peer_w_q, peer_subkeys, peer_u, peer_v):
    depth = w_in.shape[0]
    n_exp = peer_u.shape[1]
    slopes = jnp.exp2(-8.0 * (jnp.arange(N_HEADS, dtype=F32) + 1.0) / N_HEADS)
    y_p, y_s = x_prompt, x_sample
    rows_p, rows_s = [], []
    for l in range(depth):
        lam_init = 0.8 - 0.6 * math.exp(-0.3 * l)
        p = {
            "w_in": w_in[l].astype(BF16),
            "w_out": w_out[l].astype(BF16),
            "slopes": slopes,
            "lamp": jnp.stack([lam_q1[l], lam_k1[l], lam_q2[l], lam_k2[l]]),
            "subln_g": subln_g[l].reshape(1, HEAD_W),
            "ln1_g": ln1_g[l].reshape(1, D_MODEL), "ln1_b": ln1_b[l].reshape(1, D_MODEL),
            "ln2_g": ln2_g[l].reshape(1, D_TILES, LANES), "ln2_b": ln2_b[l].reshape(1, D_TILES, LANES),
            "peer_w_q": peer_w_q[l].astype(BF16),
            "subkeys": peer_subkeys[l].reshape(2 * PEER_HEADS, PEER_N_KEYS, PEER_N_KEYS).astype(BF16),
            "table": jnp.concatenate([peer_u[l].reshape(n_exp, D_TILES, LANES),
                                      peer_v[l].reshape(n_exp, D_TILES, LANES)],
                                     axis=1).reshape(n_exp * ROW_TILES, LANES),
        }
        y_p, rp = _layer(y_p, None, lam_init, p)
        past = (cache_diff_k[l], cache_diff_v[l], cache_sb_k[l], cache_sb_v[l])
        y_s, rs = _layer(y_s, past, lam_init, p)
        rows_p.append(rp)
        rows_s.append(rs)

    def stack(rows, i, tail):
        return jnp.stack([r[i].reshape(r[i].shape[:2] + tail) for r in rows])

    kshape = (N_HEADS, 2, DIFF_QK_DIM)
    vshape = (N_HEADS, HEAD_W)
    return (y_p, y_s,
            stack(rows_p, 0, kshape), stack(rows_p, 1, vshape), stack(rows_p, 2, vshape), stack(rows_p, 3, vshape),
            stack(rows_s, 0, kshape), stack(rows_s, 1, vshape), stack(rows_s, 2, vshape), stack(rows_s, 3, vshape))
```

```python
import functools
import math

import numpy as np
import jax
import jax.numpy as jnp
from jax import lax
from jax.experimental import pallas as pl
from jax.experimental.pallas import tpu as pltpu

F32 = jnp.float32
BF16 = jnp.bfloat16
I32 = jnp.int32

D_MODEL = 2048
CHUNK = 64
CHUNK_SHIFT = 6
N_HEADS = 8
HEAD_W = 128
DIFF_QK_DIM = 64
SB_HEAD_DIM = 128
GROUP_W = N_HEADS * HEAD_W
PEER_HEADS = 8
PEER_N_KEYS = 128
PEER_TOPK = 16
PEER_PICKS = PEER_HEADS * PEER_TOPK
DEPTH = 1
DEEPNORM_ALPHA = (2.0 * DEPTH) ** 0.25
LN_EPS = 1e-5
SUBLN_EPS = 1e-5

LANES = 128
SUBLANES = 8
D_TILES = D_MODEL // LANES
HALF_TILES = D_TILES // 2
ROW_TILES = 2 * HALF_TILES
ROW_PITCH = ROW_TILES + 4
VMEM_LIMIT = 56 * 1024 * 1024


def _cparams(n_axes):
    return pltpu.CompilerParams(dimension_semantics=("arbitrary",) * n_axes,
                                vmem_limit_bytes=VMEM_LIMIT)


def _dot_nt(a, b):
    return lax.dot_general(a, b, (((1,), (1,)), ((), ())), preferred_element_type=F32)


def _mm_kernel(x_ref, w_ref, o_ref):
    o_ref[...] = jnp.dot(x_ref[...].astype(BF16), w_ref[...],
                         preferred_element_type=F32).astype(o_ref.dtype)


def _matmul(x, w, col_block, width, out_dtype, tm):
    n, k = x.shape
    tm = min(tm, n)
    return pl.pallas_call(
        _mm_kernel,
        grid=(n // tm,),
        in_specs=[pl.BlockSpec((tm, k), lambda i: (i, 0)),
                  pl.BlockSpec((k, width), lambda i: (0, col_block))],
        out_specs=pl.BlockSpec((tm, width), lambda i: (i, 0)),
        out_shape=jax.ShapeDtypeStruct((n, width), out_dtype),
        compiler_params=_cparams(1), name="proj_matmul",
    )(x, w)


def _diff_attn_kernel(slopes_ref, lamp_ref, q_ref, k_ref, v_ref, g_ref, o_ref, *,
                      q_start, n_valid, tq, tk, lam_init):
    h = pl.program_id(1)
    qi = pl.program_id(2)
    slope = slopes_ref[h]
    lamp = lamp_ref[...]
    lam = (jnp.exp(jnp.sum(lamp[0:1] * lamp[1:2], axis=1, keepdims=True))
           - jnp.exp(jnp.sum(lamp[2:3] * lamp[3:4], axis=1, keepdims=True)) + lam_init)

    q = q_ref[0] * (DIFF_QK_DIM ** -0.5)
    lane = lax.broadcasted_iota(I32, (tq, HEAD_W), 1)
    qq = jnp.concatenate([jnp.where(lane < DIFF_QK_DIM, q, 0.0),
                          jnp.where(lane >= DIFF_QK_DIM, q, 0.0)], axis=0).astype(BF16)

    q_pos0 = q_start + qi * tq
    last_q = q_pos0 + tq - 1
    k_end = jnp.minimum((last_q // CHUNK + 1) * CHUNK, n_valid)
    n_kb = (k_end + tk - 1) // tk
    n_before = q_pos0 // tk
    col1 = lax.broadcasted_iota(I32, (1, tk), 1)

    def update(carry, s, vb):
        m, l, acc = carry
        m_new = jnp.maximum(m, jnp.max(s, axis=1, keepdims=True))
        p = jnp.exp(s - m_new)
        alpha = jnp.exp(m - m_new)
        l = alpha * l + jnp.sum(p, axis=1, keepdims=True)
        acc = alpha * acc + jnp.dot(p.astype(BF16), vb, preferred_element_type=F32)
        return m_new, l, acc

    def load(kj):
        koff = pl.multiple_of(kj * tk, tk)
        return (koff, k_ref[0, pl.ds(koff, tk), :].astype(BF16), v_ref[0, pl.ds(koff, tk), :].astype(BF16))

    def before_body(kj, carry):
        koff, kb, vb = load(kj)
        s = _dot_nt(qq, kb) + slope * (col1 + (koff - q_pos0)).astype(F32)
        return update(carry, s, vb)

    def edge_body(kj, carry):
        koff, kb, vb = load(kj)
        r = lax.broadcasted_iota(I32, (2 * tq, tk), 0)
        rel = jnp.where(r >= tq, r - tq, r)
        row = rel + q_pos0
        col = lax.broadcasted_iota(I32, (2 * tq, tk), 1) + koff
        mask = ((col >> CHUNK_SHIFT) <= (row >> CHUNK_SHIFT)) & (col < n_valid)
        bias = slope * (rel.astype(F32) - jnp.abs((row - col).astype(F32)))
        s = jnp.where(mask, _dot_nt(qq, kb) + bias, -jnp.inf)
        return update(carry, s, vb)

    init = (jnp.full((2 * tq, 1), -jnp.inf, F32), jnp.zeros((2 * tq, 1), F32), jnp.zeros((2 * tq, HEAD_W), F32))
    carry = lax.fori_loop(0, n_before, before_body, init)
    _, l, acc = lax.fori_loop(n_before, n_kb, edge_body, carry)
    o = acc / l
    o = o[0:tq] - lam * o[tq:2 * tq]
    o = o * lax.rsqrt(jnp.mean(o * o, axis=1, keepdims=True) + SUBLN_EPS) * g_ref[...] * (1.0 - lam_init)
    o_ref[0] = o.astype(o_ref.dtype)


def _diff_attention(q, k, v, slopes, lamp, subln_g, *, q_start, n_valid, tq, tk, lam_init):
    b, t, _ = q.shape
    lk = k.shape[1]
    kern = functools.partial(_diff_attn_kernel, q_start=q_start, n_valid=n_valid, tq=tq, tk=tk,
                             lam_init=lam_init)
    return pl.pallas_call(
        kern,
        grid=(b, N_HEADS, t // tq),
        in_specs=[pl.BlockSpec(memory_space=pltpu.SMEM),
                  pl.BlockSpec((4, DIFF_QK_DIM), lambda bi, hi, qi: (0, 0)),
                  pl.BlockSpec((1, tq, HEAD_W), lambda bi, hi, qi: (bi, qi, hi)),
                  pl.BlockSpec((1, lk, HEAD_W), lambda bi, hi, qi: (bi, 0, hi)),
                  pl.BlockSpec((1, lk, HEAD_W), lambda bi, hi, qi: (bi, 0, hi)),
                  pl.BlockSpec((1, HEAD_W), lambda bi, hi, qi: (0, 0))],
        out_specs=pl.BlockSpec((1, tq, HEAD_W), lambda bi, hi, qi: (bi, qi, hi)),
        out_shape=jax.ShapeDtypeStruct((b, t, GROUP_W), BF16),
        compiler_params=_cparams(3), name="diff_attention",
    )(slopes, lamp, q, k, v, subln_g)


def _sb_attn_kernel(q_ref, k_ref, v_ref, o_ref, *, q_start, tq, tk):
    qi = pl.program_id(2)
    q = (q_ref[0] * (SB_HEAD_DIM ** -0.5)).astype(BF16)
    q_pos0 = q_start + qi * tq
    last_q = q_pos0 + tq - 1
    n_kb = (last_q + tk - 1) // tk
    n_before = q_pos0 // tk
    tri = (lax.broadcasted_iota(I32, (tk, tk), 0) > lax.broadcasted_iota(I32, (tk, tk), 1)).astype(BF16)

    def step(kj, carry, masked):
        later, acc = carry
        koff = pl.multiple_of(kj * tk, tk)
        kb = k_ref[0, pl.ds(koff, tk), :].astype(BF16)
        vb = v_ref[0, pl.ds(koff, tk), :].astype(BF16)
        z = _dot_nt(q, kb)
        log_beta = jnp.minimum(z, 0.0) - jnp.log1p(jnp.exp(-jnp.abs(z)))
        log_keep = log_beta - z
        if masked:
            row = lax.broadcasted_iota(I32, (tq, tk), 0) + q_pos0
            mask = (lax.broadcasted_iota(I32, (tq, tk), 1) + koff) < row
            log_keep = jnp.where(mask, log_keep, 0.0)
        hi = log_keep.astype(BF16)
        lo = (log_keep - hi.astype(F32)).astype(BF16)
        in_block = (jnp.dot(hi, tri, preferred_element_type=F32)
                    + jnp.dot(lo, tri, preferred_element_type=F32))
        a = jnp.exp(log_beta + in_block + later)
        if masked:
            a = jnp.where(mask, a, 0.0)
        acc = acc + jnp.dot(a.astype(BF16), vb, preferred_element_type=F32)
        later = later + jnp.sum(log_keep, axis=1, keepdims=True)
        return later, acc

    init = (jnp.zeros((tq, 1), F32), jnp.zeros((tq, HEAD_W), F32))
    carry = lax.fori_loop(0, n_kb - n_before, lambda i, c: step(n_kb - 1 - i, c, True), init)
    _, acc = lax.fori_loop(0, n_before, lambda i, c: step(n_before - 1 - i, c, False), carry)
    o_ref[0] = acc.astype(o_ref.dtype)


def _sb_attention(q, k, v, *, q_start, tq, tk):
    b, t, _ = q.shape
    lk = k.shape[1]
    kern = functools.partial(_sb_attn_kernel, q_start=q_start, tq=tq, tk=tk)
    return pl.pallas_call(
        kern,
        grid=(b, N_HEADS, t // tq),
        in_specs=[pl.BlockSpec((1, tq, HEAD_W), lambda bi, hi, qi: (bi, qi, hi)),
                  pl.BlockSpec((1, lk, HEAD_W), lambda bi, hi, qi: (bi, 0, hi)),
                  pl.BlockSpec((1, lk, HEAD_W), lambda bi, hi, qi: (bi, 0, hi))],
        out_specs=pl.BlockSpec((1, tq, HEAD_W), lambda bi, hi, qi: (bi, qi, hi)),
        out_shape=jax.ShapeDtypeStruct((b, t, GROUP_W), BF16),
        compiler_params=_cparams(3), name="sb_attention",
    )(q, k, v)


def _layer_norm(y, g, b):
    mu = jnp.mean(y, axis=-1, keepdims=True)
    d = y - mu
    var = jnp.mean(d * d, axis=-1, keepdims=True)
    return d * lax.rsqrt(var + LN_EPS) * g + b


def _outproj_ln_kernel(x_ref, od_ref, os_ref, w_ref, g_ref, b_ref, o_ref):
    y = (DEEPNORM_ALPHA * x_ref[...]
         + jnp.dot(od_ref[...], w_ref[0:GROUP_W, :], preferred_element_type=F32)
         + jnp.dot(os_ref[...], w_ref[GROUP_W:2 * GROUP_W, :], preferred_element_type=F32))
    o_ref[...] = _layer_norm(y, g_ref[...], b_ref[...])


def _outproj_ln(x, o_diff, o_sb, w_out, g, b, tm):
    n = x.shape[0]
    tm = min(tm, n)
    return pl.pallas_call(
        _outproj_ln_kernel,
        grid=(n // tm,),
        in_specs=[pl.BlockSpec((tm, D_MODEL), lambda i: (i, 0)),
                  pl.BlockSpec((tm, GROUP_W), lambda i: (i, 0)),
                  pl.BlockSpec((tm, GROUP_W), lambda i: (i, 0)),
                  pl.BlockSpec((2 * GROUP_W, D_MODEL), lambda i: (0, 0)),
                  pl.BlockSpec((1, D_MODEL), lambda i: (0, 0)),
                  pl.BlockSpec((1, D_MODEL), lambda i: (0, 0))],
        out_specs=pl.BlockSpec((tm, D_MODEL), lambda i: (i, 0)),
        out_shape=jax.ShapeDtypeStruct((n, D_MODEL), F32),
        compiler_params=_cparams(1), name="outproj_ln",
    )(x, o_diff, o_sb, w_out, g, b)


_CAND_NB = [PEER_TOPK // (a + 1) for a in range(PEER_TOPK)]
_CAND_ROWS = sum(_CAND_NB)
_CAND_PAD = -(-_CAND_ROWS // SUBLANES) * SUBLANES
_NO_CAND = float(PEER_TOPK * PEER_TOPK)
ROUTE_HEAD_GROUP = 4


def _cand_flat_index(tm):
    flat = [a * PEER_TOPK + b for a in range(PEER_TOPK) for b in range(_CAND_NB[a])]
    flat += [_NO_CAND] * (_CAND_PAD - _CAND_ROWS)
    return jnp.asarray(np.broadcast_to(np.asarray(flat, np.float32)[:, None], (_CAND_PAD, tm)))


def _extract_topk(scores, ids, vals_ref, idx_ref, bases, no_id):
    def body(it, carry):
        out = []
        for s, rid, base in zip(carry, ids, bases):
            m = jnp.max(s, axis=0, keepdims=True)
            idx = jnp.min(jnp.where(s == m, rid, no_id), axis=0, keepdims=True)
            vals_ref[pl.ds(base + it, 1), :] = m
            idx_ref[pl.ds(base + it, 1), :] = idx
            out.append(jnp.where(rid == idx, -jnp.inf, s))
        return tuple(out)

    lax.fori_loop(0, PEER_TOPK, body, tuple(scores))


def _route_kernel(q_ref, sub_ref, flat_ref, eidx_ref, gate_ref, hv_ref, hi_ref, cand_ref, cv_ref, ci_ref,
                  tv_ref, te_ref):
    tm = q_ref.shape[0]
    key_id = lax.broadcasted_iota(I32, (PEER_N_KEYS, tm), 0).astype(F32)
    k16 = lax.broadcasted_iota(I32, (PEER_TOPK, tm), 0).astype(F32)
    two_k = 2 * PEER_TOPK
    for h in range(PEER_HEADS):
        scores = [_dot_nt(sub_ref[2 * h + c], q_ref[:, (2 * h + c) * PEER_N_KEYS:(2 * h + c + 1) * PEER_N_KEYS])
                  for c in range(2)]
        _extract_topk(scores, [key_id, key_id], hv_ref, hi_ref,
                      [h * two_k, h * two_k + PEER_TOPK], float(PEER_N_KEYS))
    flat = flat_ref[...]
    for h0 in range(0, PEER_HEADS, ROUTE_HEAD_GROUP):
        cands = []
        for g in range(ROUTE_HEAD_GROUP):
            base = (h0 + g) * two_k
            cand_ref[g, _CAND_PAD - SUBLANES:_CAND_PAD, :] = jnp.full((SUBLANES, tm), -jnp.inf, F32)
            off = 0
            for a in range(PEER_TOPK):
                nb = _CAND_NB[a]
                cand_ref[g, off:off + nb, :] = (hv_ref[base + a:base + a + 1, :]
                                                + hv_ref[base + PEER_TOPK:base + PEER_TOPK + nb, :])
                off += nb
            cands.append(cand_ref[g])
        _extract_topk(cands, [flat] * ROUTE_HEAD_GROUP, cv_ref, ci_ref,
                      [(h0 + g) * PEER_TOPK for g in range(ROUTE_HEAD_GROUP)], _NO_CAND)
    for h in range(PEER_HEADS):
        base = h * two_k
        i1, i2 = hi_ref[base:base + PEER_TOPK, :], hi_ref[base + PEER_TOPK:base + two_k, :]
        pos = ci_ref[h * PEER_TOPK:(h + 1) * PEER_TOPK, :]
        top = cv_ref[h * PEER_TOPK:(h + 1) * PEER_TOPK, :]
        rows = []
        for j in range(PEER_TOPK):
            pa = jnp.floor(pos[j:j + 1, :] * (1.0 / PEER_TOPK))
            pb = pos[j:j + 1, :] - pa * PEER_TOPK
            ea = jnp.sum(jnp.where(k16 == pa, i1, 0.0), axis=0, keepdims=True)
            eb = jnp.sum(jnp.where(k16 == pb, i2, 0.0), axis=0, keepdims=True)
            rows.append(ea * PEER_N_KEYS + eb)
        e = jnp.exp(top - top[0:1, :])
        tv_ref[h * PEER_TOPK:(h + 1) * PEER_TOPK, :] = e / jnp.sum(e, axis=0, keepdims=True)
        te_ref[h * PEER_TOPK:(h + 1) * PEER_TOPK, :] = jnp.concatenate(rows, axis=0) * ROW_TILES
    gate_ref[...] = tv_ref[...].T
    eidx_ref[...] = te_ref[...].T.astype(I32)


def _peer_route(qp, subkeys, tm):
    n = qp.shape[0]
    return pl.pallas_call(
        _route_kernel,
        grid=(n // tm,),
        in_specs=[pl.BlockSpec((tm, PEER_HEADS * 2 * PEER_N_KEYS), lambda i: (i, 0)),
                  pl.BlockSpec((2 * PEER_HEADS, PEER_N_KEYS, PEER_N_KEYS), lambda i: (0, 0, 0)),
                  pl.BlockSpec((_CAND_PAD, tm), lambda i: (0, 0))],
        out_specs=[pl.BlockSpec((tm, PEER_PICKS), lambda i: (i, 0)),
                   pl.BlockSpec((tm, PEER_PICKS), lambda i: (i, 0))],
        out_shape=[jax.ShapeDtypeStruct((n, PEER_PICKS), I32),
                   jax.ShapeDtypeStruct((n, PEER_PICKS), F32)],
        scratch_shapes=[pltpu.VMEM((PEER_HEADS * 2 * PEER_TOPK, tm), F32),
                        pltpu.VMEM((PEER_HEADS * 2 * PEER_TOPK, tm), F32),
                        pltpu.VMEM((ROUTE_HEAD_GROUP, _CAND_PAD, tm), F32),
                        pltpu.VMEM((PEER_PICKS, tm), F32), pltpu.VMEM((PEER_PICKS, tm), F32),
                        pltpu.VMEM((PEER_PICKS, tm), F32), pltpu.VMEM((PEER_PICKS, tm), F32)],
        compiler_params=_cparams(1), name="peer_route",
    )(qp, subkeys, _cand_flat_index(tm))


N_SLOTS = 8
GATHER_AHEAD = N_SLOTS - 3
ISSUE_PER_STEP = PEER_PICKS // HALF_TILES
PAIRS = 2 * PEER_PICKS


def _expert_kernel(eidx_ref, gate_ref, x_ref, tab_ref, g_ref, b_ref, o_ref, buf_ref, po_ref, sem_ref):
    tb = x_ref.shape[0]

    def row_copy(tok, r, slot):
        off = pl.multiple_of(eidx_ref[tok, r], ROW_TILES)
        return pltpu.make_async_copy(tab_ref.at[pl.ds(off, ROW_TILES), :],
                                     buf_ref.at[slot, pl.ds(r * ROW_PITCH, ROW_TILES), :],
                                     sem_ref.at[slot])

    def wait_slot(slot):
        pltpu.make_async_copy(tab_ref.at[pl.ds(0, PEER_PICKS * ROW_TILES), :],
                              buf_ref.at[slot, pl.ds(0, PEER_PICKS * ROW_TILES), :],
                              sem_ref.at[slot]).wait()

    lane = lax.broadcasted_iota(I32, (LANES, LANES), 1)
    pad_rows = jnp.zeros((LANES - D_TILES, LANES), F32)
    act_rows = 2 * SUBLANES
    pair_row = lax.broadcasted_iota(I32, (act_rows, PAIRS), 0)
    pair_even = (lax.broadcasted_iota(I32, (act_rows, PAIRS), 1) & 1) == 0
    lane_even = (lax.broadcasted_iota(I32, (act_rows, LANES), 1) & 1) == 0

    def x_weights(tok):
        return jnp.concatenate([x_ref[tok], pad_rows], axis=0).T

    def up_step(slot, xt, s, acc):
        u2 = pltpu.bitcast(buf_ref[slot, pl.ds(s, PEER_PICKS, stride=ROW_PITCH), :], BF16)
        w_s = jnp.where((lane == s) | (lane == s + HALF_TILES), xt, 0.0).astype(BF16)
        return acc + jnp.dot(u2, w_s, preferred_element_type=F32)

    def swap_pairs(t):
        return jnp.where(lane_even, pltpu.roll(t, LANES - 1, 1), pltpu.roll(t, 1, 1))

    def activation(tok, acc):
        at = acc.T
        halves = []
        for c in range(PAIRS // LANES):
            a_c = at[:, c * LANES:(c + 1) * LANES]
            low = jnp.sum(a_c[0:HALF_TILES, :], axis=0, keepdims=True)
            high = jnp.sum(a_c[HALF_TILES:D_TILES, :], axis=0, keepdims=True)
            t = jnp.where(lane_even, low, high)
            halves.append(t + swap_pairs(t))
        pre = jnp.concatenate(halves, axis=1)
        act = 0.5 * pre * (1.0 + lax.erf(pre * (2.0 ** -0.5))) * gate_ref[pl.ds(tok, 1), :]
        keep = ((pair_row == 0) & pair_even) | ((pair_row == 1) & jnp.logical_not(pair_even))
        return jnp.where(keep, act, 0.0).astype(BF16)

    def down_step(slot, tok, act, s):
        v2 = pltpu.bitcast(buf_ref[slot, pl.ds(HALF_TILES + s, PEER_PICKS, stride=ROW_PITCH), :], BF16)
        o_s = jnp.dot(act, v2, preferred_element_type=F32)
        po_ref[tok, pl.ds(s, 2, stride=HALF_TILES), :] = o_s[0:2, :]

    def issue(tok, rows):
        nxt = jnp.minimum(tok + GATHER_AHEAD, tb - 1)
        nslot = lax.rem(tok + GATHER_AHEAD, N_SLOTS)
        for r in rows:
            row_copy(nxt, r, nslot).start(priority=r % 2)

    def up_start(tok):
        wait_slot(lax.rem(tok, N_SLOTS))
        return x_weights(tok), jnp.zeros((PAIRS, LANES), F32)

    for tok in range(GATHER_AHEAD):
        for r in range(PEER_PICKS):
            row_copy(tok, r, tok).start(priority=r % 2)

    accs = []
    for tok in range(2):
        issue(tok, range(PEER_PICKS))
        xt, acc = up_start(tok)
        for s in range(HALF_TILES):
            acc = up_step(tok, xt, s, acc)
        accs.append(acc)
    act0 = activation(0, accs[0])

    def token(tok, carry):
        acc_prev, act_prev2 = carry
        slot = lax.rem(tok, N_SLOTS)
        slot2 = lax.rem(tok - 2, N_SLOTS)
        xt, acc = up_start(tok)
        act_prev = activation(tok - 1, acc_prev)
        for s in range(HALF_TILES):
            acc = up_step(slot, xt, s, acc)
            down_step(slot2, tok - 2, act_prev2, s)
            issue(tok, range(s * ISSUE_PER_STEP, (s + 1) * ISSUE_PER_STEP))
        return acc, act_prev

    acc_last, act_prev2 = lax.fori_loop(2, tb, token, (accs[1], act0))
    act_last = activation(tb - 1, acc_last)
    for s in range(HALF_TILES):
        down_step((tb - 2) % N_SLOTS, tb - 2, act_prev2, s)
        down_step((tb - 1) % N_SLOTS, tb - 1, act_last, s)
    for i in range(GATHER_AHEAD):
        wait_slot((tb + i) % N_SLOTS)

    y = DEEPNORM_ALPHA * x_ref[...] + po_ref[...]
    inv_d = 1.0 / D_MODEL
    mu = jnp.sum(jnp.sum(y, axis=2, keepdims=True), axis=1, keepdims=True) * inv_d
    d = y - mu
    var = jnp.sum(jnp.sum(d * d, axis=2, keepdims=True), axis=1, keepdims=True) * inv_d
    o_ref[...] = d * lax.rsqrt(var + LN_EPS) * g_ref[...] + b_ref[...]


def _peer_experts(eidx, gates, x3, table, g3, b3, tb):
    n = x3.shape[0]
    tb = min(tb, n)
    tok_spec = pl.BlockSpec((tb, D_TILES, LANES), lambda i: (i, 0, 0))
    par_spec = pl.BlockSpec((1, D_TILES, LANES), lambda i: (0, 0, 0))
    return pl.pallas_call(
        _expert_kernel,
        grid=(n // tb,),
        in_specs=[pl.BlockSpec((tb, PEER_PICKS), lambda i: (i, 0), memory_space=pltpu.SMEM),
                  pl.BlockSpec((tb, PAIRS), lambda i: (i, 0)),
                  tok_spec,
                  pl.BlockSpec(memory_space=pl.ANY),
                  par_spec, par_spec],
        out_specs=tok_spec,
        out_shape=jax.ShapeDtypeStruct((n, D_TILES, LANES), F32),
        scratch_shapes=[pltpu.VMEM((N_SLOTS, PEER_PICKS * ROW_PITCH, LANES), I32),
                        pltpu.VMEM((tb, D_TILES, LANES), F32),
                        pltpu.SemaphoreType.DMA((N_SLOTS,))],
        compiler_params=_cparams(1), name="peer_experts",
    )(eidx, gates, x3, table, g3, b3)


def _pack_halves(tab):
    bits = lax.bitcast_convert_type(tab.astype(BF16), jnp.uint16).astype(jnp.uint32)
    bits = bits.reshape(tab.shape[0], 2, HALF_TILES, LANES)
    return lax.bitcast_convert_type(bits[:, 0] | (bits[:, 1] << 16), I32)


def _pad_keys(past, new, tk):
    cat = jnp.concatenate([past, new], axis=1)
    pad = (-cat.shape[1]) % tk
    return jnp.pad(cat, ((0, 0), (0, pad), (0, 0)))


def _layer(x, past, lam_init, p):
    b, t, _ = x.shape
    n = b * t
    xf = x.reshape(n, D_MODEL)
    tm = 512
    qd, kd, vd, qs, ks, vs = [_matmul(xf, p["w_in"], j, GROUP_W, F32, tm).reshape(b, t, GROUP_W)
                              for j in range(6)]
    tk = 512
    if past is None:
        q_start, tq = 0, 512
        keys = (kd, vd, ks, vs)
    else:
        q_start, tq = past[0].shape[1], t
        keys = tuple(_pad_keys(c.reshape(b, q_start, GROUP_W), r, tk) for c, r in zip(past, (kd, vd, ks, vs)))
    o_diff = _diff_attention(qd, keys[0], keys[1], p["slopes"], p["lamp"], p["subln_g"],
                             q_start=q_start, n_valid=q_start + t, tq=tq, tk=tk, lam_init=lam_init)
    o_sb = _sb_attention(qs, keys[2], keys[3], q_start=q_start, tq=tq, tk=tk)
    x1 = _outproj_ln(xf, o_diff.reshape(n, GROUP_W), o_sb.reshape(n, GROUP_W), p["w_out"],
                     p["ln1_g"], p["ln1_b"], 256)
    qp = _matmul(x1, p["peer_w_q"], 0, PEER_HEADS * 2 * PEER_N_KEYS, BF16, tm)
    eidx, gates = _peer_route(qp, p["subkeys"], 128)
    y3 = _peer_experts(eidx, jnp.repeat(gates, 2, axis=1), x1.reshape(n, D_TILES, LANES), p["table"],
                       p["ln2_g"], p["ln2_b"], 128)
    return y3.reshape(b, t, D_MODEL), (kd, vd, ks, vs)


def kernel(x_prompt, x_sample, cache_diff_k, cache_diff_v, cache_sb_k, cache_sb_v, w_in, w_out, lam_q1, lam_k1, lam_q2, lam_k2, subln_g, ln1_g, ln1_b, ln2_g, ln2_b, peer_w_q, peer_subkeys, peer_u, peer_v):
    depth = w_in.shape[0]
    n_exp = peer_u.shape[1]
    slopes = jnp.exp2(-8.0 * (jnp.arange(N_HEADS, dtype=F32) + 1.0) / N_HEADS)
    y_p, y_s = x_prompt, x_sample
    rows_p, rows_s = [], []
    for l in range(depth):
        lam_init = 0.8 - 0.6 * math.exp(-0.3 * l)
        p = {
            "w_in": w_in[l].astype(BF16),
            "w_out": w_out[l].astype(BF16),
            "slopes": slopes,
            "lamp": jnp.stack([lam_q1[l], lam_k1[l], lam_q2[l], lam_k2[l]]),
            "subln_g": subln_g[l].reshape(1, HEAD_W),
            "ln1_g": ln1_g[l].reshape(1, D_MODEL), "ln1_b": ln1_b[l].reshape(1, D_MODEL),
            "ln2_g": ln2_g[l].reshape(1, D_TILES, LANES), "ln2_b": ln2_b[l].reshape(1, D_TILES, LANES),
            "peer_w_q": peer_w_q[l].astype(BF16),
            "subkeys": peer_subkeys[l].reshape(2 * PEER_HEADS, PEER_N_KEYS, PEER_N_KEYS).astype(BF16),
            "table": jnp.concatenate([_pack_halves(peer_u[l]), _pack_halves(peer_v[l])],
                                     axis=1).reshape(n_exp * ROW_TILES, LANES),
        }
        y_p, rp = _layer(y_p, None, lam_init, p)
        past = (cache_diff_k[l], cache_diff_v[l], cache_sb_k[l], cache_sb_v[l])
        y_s, rs = _layer(y_s, past, lam_init, p)
        rows_p.append(rp)
        rows_s.append(rs)

    def stack(rows, i, tail):
        return jnp.stack([r[i].reshape(r[i].shape[:2] + tail) for r in rows])

    kshape = (N_HEADS, 2, DIFF_QK_DIM)
    vshape = (N_HEADS, HEAD_W)
    return (y_p, y_s,
            stack(rows_p, 0, kshape), stack(rows_p, 1, vshape), stack(rows_p, 2, vshape), stack(rows_p, 3, vshape),
            stack(rows_s, 0, kshape), stack(rows_s, 1, vshape), stack(rows_s, 2, vshape), stack(rows_s, 3, vshape))
```
